```python
import math, functools
import jax, jax.numpy as jnp
from jax import lax
import numpy as np

D_MODEL = 1024
BATCH = 8
SEQ = 2048
DEPTH = 4
DEC_BATCH = 128
DEC_SEQ = 1
PAST_LEN = 16384
PAGE_SIZE = 128

SSD_HEADS = 16
SSD_HEAD_DIM = 64
SSD_INNER = SSD_HEADS * SSD_HEAD_DIM
SSD_GROUPS = 2
SSD_STATE = 128
SSD_CONV_DIM = SSD_INNER + 2 * SSD_GROUPS * SSD_STATE
CONV_WIDTH = 4
CHUNK = 128
LRU_HEADS = 16
LRU_BLOCK = 64
LRU_WIDTH = LRU_HEADS * LRU_BLOCK
LRU_C = 8.0
MIX_WIDTH = SSD_INNER + LRU_WIDTH
IN_WIDTH = SSD_INNER + SSD_CONV_DIM + SSD_HEADS + 2 * LRU_WIDTH
D_FF = 2816
N_EXPERTS = 8
TOP_K = 2
D_FF_EXPERT = 3584
N_DENSE = (DEPTH + 1) // 2
N_MOE = DEPTH // 2
EPS = 1e-6

kernel_name = "hymba_ssd_rglru_moe_step"


def rmsnorm(x, g):
    xf = x.astype(jnp.float32)
    y = xf * lax.rsqrt(jnp.mean(xf * xf, axis=-1, keepdims=True) + EPS)
    return (y * g.astype(jnp.float32)).astype(x.dtype)


def causal_conv(x, buf, w, b):
    L = x.shape[1]
    xp = jnp.concatenate([buf.astype(x.dtype), x], axis=1)
    y = b + sum(xp[:, k:k + L] * w[k] for k in range(CONV_WIDTH))
    return y, xp[:, -(CONV_WIDTH - 1):]


def ssd_chunked(x, dt, A, Bm, Cm, h0):
    b, L = x.shape[:2]
    q = min(CHUNK, L)
    nc = -(-L // q)
    pad = nc * q - L
    if pad:
        padf = lambda t: jnp.pad(t, [(0, 0), (0, pad)] + [(0, 0)] * (t.ndim - 2))
        x, dt, Bm, Cm = padf(x), padf(dt), padf(Bm), padf(Cm)
    G, J = SSD_GROUPS, SSD_HEADS // SSD_GROUPS
    xc = x.reshape(b, nc, q, G, J, SSD_HEAD_DIM)
    dtc = dt.reshape(b, nc, q, G, J)
    Bc = Bm.reshape(b, nc, q, G, SSD_STATE)
    Cc = Cm.reshape(b, nc, q, G, SSD_STATE)
    cum = jnp.cumsum(dtc * A.reshape(G, J), axis=2)
    tri = jnp.tril(jnp.ones((q, q), dtype=bool))
    seg = cum[:, :, :, None] - cum[:, :, None, :]
    decay = jnp.exp(jnp.where(tri[:, :, None, None], seg, -jnp.inf))
    cb = jnp.einsum('bctgn,bcsgn->bctsg', Cc, Bc)
    dtx = dtc[..., None] * xc
    y_intra = jnp.einsum('bctsg,bctsgj,bcsgjp->bctgjp', cb, decay, dtx)
    dec_end = jnp.exp(cum[:, :, -1:] - cum)
    states = jnp.einsum('bcsgn,bcsgj,bcsgjp->bcgjpn', Bc, dec_end, dtx)
    chunk_decay = jnp.exp(cum[:, :, -1])

    def step(h, inp):
        st, dcy = inp
        return h * dcy[..., None, None] + st, h

    h0g = h0.reshape(b, G, J, SSD_HEAD_DIM, SSD_STATE)
    h_last, h_prev = lax.scan(step, h0g, (jnp.moveaxis(states, 1, 0), jnp.moveaxis(chunk_decay, 1, 0)))
    h_prev = jnp.moveaxis(h_prev, 0, 1)
    y_inter = jnp.einsum('bctgn,bcgjpn,bctgj->bctgjp', Cc, h_prev, jnp.exp(cum))
    y = (y_intra + y_inter).reshape(b, nc * q, SSD_HEADS, SSD_HEAD_DIM)[:, :L]
    return y, h_last.reshape(b, SSD_HEADS, SSD_HEAD_DIM, SSD_STATE)


def ssd_mixer(z, xbc, dt_raw, conv_buf, h0, conv_w, conv_b, dt_bias, A_log, D_skip, norm_g):
    f32 = jnp.float32
    b, L = z.shape[:2]
    xbc, new_buf = causal_conv(xbc, conv_buf, conv_w, conv_b)
    xbc = jax.nn.silu(xbc)
    xs, Bm, Cm = jnp.split(xbc, [SSD_INNER, SSD_INNER + SSD_GROUPS * SSD_STATE], axis=-1)
    xs = xs.reshape(b, L, SSD_HEADS, SSD_HEAD_DIM).astype(f32)
    Bm = Bm.reshape(b, L, SSD_GROUPS, SSD_STATE).astype(f32)
    Cm = Cm.reshape(b, L, SSD_GROUPS, SSD_STATE).astype(f32)
    dt = jax.nn.softplus(dt_raw.astype(f32) + dt_bias.astype(f32))
    A = -jnp.exp(A_log.astype(f32))
    y, h_last = ssd_chunked(xs, dt, A, Bm, Cm, h0.astype(f32))
    y = y + D_skip.astype(f32)[:, None] * xs
    y = y.reshape(b, L, SSD_INNER) * jax.nn.silu(z.astype(f32))
    yg = y.reshape(b, L, SSD_GROUPS, SSD_INNER // SSD_GROUPS)
    yg = yg * lax.rsqrt(jnp.mean(yg * yg, axis=-1, keepdims=True) + EPS)
    y = yg.reshape(b, L, SSD_INNER) * norm_g.astype(f32)
    return y.astype(z.dtype), new_buf, h_last


def rglru_mixer(xl, gate, conv_buf, h0, conv_w, conv_b, w_a, b_a, w_x, b_x, lam, norm_g, reset_first):
    f32 = jnp.float32
    b, L = xl.shape[:2]
    xc, new_buf = causal_conv(xl, conv_buf, conv_w, conv_b)
    xb = xc.reshape(b, L, LRU_HEADS, LRU_BLOCK)
    r = jax.nn.sigmoid(jnp.einsum('blhi,hij->blhj', xb, w_a).reshape(b, L, LRU_WIDTH) + b_a)
    i = jax.nn.sigmoid(jnp.einsum('blhi,hij->blhj', xb, w_x).reshape(b, L, LRU_WIDTH) + b_x)
    log_a = -LRU_C * r.astype(f32) * jax.nn.softplus(-lam.astype(f32))
    a = jnp.exp(log_a)
    mult = jnp.sqrt(-jnp.expm1(2.0 * log_a))
    if reset_first:
        mult = mult.at[:, 0].set(1.0)
    u = mult * (i * xc).astype(f32)
    u = u.at[:, 0].add(a[:, 0] * h0.astype(f32))

    def combine(left, right):
        a1, b1 = left
        a2, b2 = right
        return a1 * a2, a2 * b1 + b2

    _, h = lax.associative_scan(combine, (a, u), axis=1)
    y = h * jax.nn.gelu(gate.astype(f32))
    y = y * lax.rsqrt(jnp.mean(y * y, axis=-1, keepdims=True) + EPS) * norm_g.astype(f32)
    return y.astype(xl.dtype), new_buf, h[:, -1]


def swiglu(x, wg, wu, wd):
    return (jax.nn.silu(x @ wg) * (x @ wu)) @ wd


def moe_swiglu(x, w_r, wg, wu, wd):
    logits = (x @ w_r).astype(jnp.float32)
    top_v, top_i = lax.top_k(logits, TOP_K)
    gates = jax.nn.softmax(top_v, axis=-1)
    dense_gate = jnp.sum(jax.nn.one_hot(top_i, N_EXPERTS, dtype=jnp.float32) * gates[..., None], axis=-2)
    out = jnp.zeros_like(x)
    for e in range(N_EXPERTS):
        out = out + dense_gate[..., e:e + 1].astype(x.dtype) * swiglu(x, wg[e], wu[e], wd[e])
    return out


def run_trunk(x, states, P, reset_first):
    b = x.shape[0]
    new_ssd, new_ssd_conv, new_lru, new_lru_conv = [], [], [], []
    split_pts = [SSD_INNER, SSD_INNER + SSD_CONV_DIM, SSD_INNER + SSD_CONV_DIM + SSD_HEADS,
                 SSD_INNER + SSD_CONV_DIM + SSD_HEADS + LRU_WIDTH]
    for l in range(DEPTH):
        if states is None:
            hs = jnp.zeros((b, SSD_HEADS, SSD_HEAD_DIM, SSD_STATE), jnp.float32)
            cs = jnp.zeros((b, CONV_WIDTH - 1, SSD_CONV_DIM), x.dtype)
            hl = jnp.zeros((b, LRU_WIDTH), jnp.float32)
            cl = jnp.zeros((b, CONV_WIDTH - 1, LRU_WIDTH), x.dtype)
        else:
            hs, cs, hl, cl = states[0][l], states[1][l], states[2][l], states[3][l]
        hn = rmsnorm(x, P['ln_mix'][l])
        proj = hn @ P['w_in'][l]
        z, xbc, dt_raw, xl, gate = jnp.split(proj, split_pts, axis=-1)
        y_ssd, cs_new, hs_new = ssd_mixer(z, xbc, dt_raw, cs, hs, P['ssd_conv_w'][l], P['ssd_conv_b'][l],
                                          P['ssd_dt_bias'][l], P['ssd_A_log'][l], P['ssd_D'][l], P['ssd_norm'][l])
        y_lru, cl_new, hl_new = rglru_mixer(xl, gate, cl, hl, P['lru_conv_w'][l], P['lru_conv_b'][l],
                                            P['lru_w_a'][l], P['lru_b_a'][l], P['lru_w_x'][l], P['lru_b_x'][l],
                                            P['lru_lambda'][l], P['lru_norm'][l], reset_first)
        x = x + jnp.concatenate([y_ssd, y_lru], axis=-1) @ P['w_out'][l]
        hn = rmsnorm(x, P['ln_ffn'][l])
        if l % 2 == 0:
            k = l // 2
            x = x + swiglu(hn, P['ffn_w_gate'][k], P['ffn_w_up'][k], P['ffn_w_down'][k])
        else:
            k = l // 2
            x = x + moe_swiglu(hn, P['moe_w_router'][k], P['moe_w_gate'][k], P['moe_w_up'][k], P['moe_w_down'][k])
        new_ssd.append(hs_new)
        new_ssd_conv.append(cs_new)
        new_lru.append(hl_new)
        new_lru_conv.append(cl_new)
    y = rmsnorm(x, P['ln_final'])
    return y, jnp.stack(new_ssd), jnp.stack(new_ssd_conv), jnp.stack(new_lru), jnp.stack(new_lru_conv)


def setup_inputs(seed: int = 0) -> dict:
    key = jax.random.key(seed)
    ks = iter(jax.random.split(key, 40))
    nrm = lambda shape, s: jax.random.normal(next(ks), shape, jnp.float32) * s
    gain = lambda shape: 1.0 + nrm(shape, 0.01)
    A_log = jnp.log(jax.random.uniform(next(ks), (DEPTH, SSD_HEADS), jnp.float32, 1.0, 16.0))
    dt0 = jnp.exp(jax.random.uniform(next(ks), (DEPTH, SSD_HEADS), jnp.float32, math.log(1e-3), math.log(1e-1)))
    dt_bias = dt0 + jnp.log(-jnp.expm1(-dt0))
    a_c = jax.random.uniform(next(ks), (DEPTH, LRU_WIDTH), jnp.float32, 0.9, 0.999)
    s = a_c ** (1.0 / LRU_C)
    lam = jnp.log(s) - jnp.log1p(-s)
    return {
        'x_prompt': nrm((BATCH, SEQ, D_MODEL), 1.0),
        'x_sample': nrm((DEC_BATCH, DEC_SEQ, D_MODEL), 1.0),
        'state_ssd': nrm((DEPTH, DEC_BATCH, SSD_HEADS, SSD_HEAD_DIM, SSD_STATE), 0.5),
        'state_ssd_conv': nrm((DEPTH, DEC_BATCH, CONV_WIDTH - 1, SSD_CONV_DIM), 1.0),
        'state_lru': nrm((DEPTH, DEC_BATCH, LRU_WIDTH), 1.0),
        'state_lru_conv': nrm((DEPTH, DEC_BATCH, CONV_WIDTH - 1, LRU_WIDTH), 1.0),
        'ln_mix': gain((DEPTH, D_MODEL)),
        'w_in': nrm((DEPTH, D_MODEL, IN_WIDTH), D_MODEL ** -0.5),
        'ssd_conv_w': nrm((DEPTH, CONV_WIDTH, SSD_CONV_DIM), CONV_WIDTH ** -0.5),
        'ssd_conv_b': nrm((DEPTH, SSD_CONV_DIM), 0.01),
        'ssd_dt_bias': dt_bias,
        'ssd_A_log': A_log,
        'ssd_D': 1.0 + nrm((DEPTH, SSD_HEADS), 0.1),
        'ssd_norm': gain((DEPTH, SSD_INNER)),
        'lru_conv_w': nrm((DEPTH, CONV_WIDTH, LRU_WIDTH), CONV_WIDTH ** -0.5),
        'lru_conv_b': nrm((DEPTH, LRU_WIDTH), 0.01),
        'lru_w_a': nrm((DEPTH, LRU_HEADS, LRU_BLOCK, LRU_BLOCK), LRU_BLOCK ** -0.5),
        'lru_b_a': nrm((DEPTH, LRU_WIDTH), 0.01),
        'lru_w_x': nrm((DEPTH, LRU_HEADS, LRU_BLOCK, LRU_BLOCK), LRU_BLOCK ** -0.5),
        'lru_b_x': nrm((DEPTH, LRU_WIDTH), 0.01),
        'lru_lambda': lam,
        'lru_norm': gain((DEPTH, LRU_WIDTH)),
        'w_out': nrm((DEPTH, MIX_WIDTH, D_MODEL), MIX_WIDTH ** -0.5),
        'ln_ffn': gain((DEPTH, D_MODEL)),
        'ffn_w_gate': nrm((N_DENSE, D_MODEL, D_FF), D_MODEL ** -0.5),
        'ffn_w_up': nrm((N_DENSE, D_MODEL, D_FF), D_MODEL ** -0.5),
        'ffn_w_down': nrm((N_DENSE, D_FF, D_MODEL), D_FF ** -0.5),
        'moe_w_router': nrm((N_MOE, D_MODEL, N_EXPERTS), D_MODEL ** -0.5),
        'moe_w_gate': nrm((N_MOE, N_EXPERTS, D_MODEL, D_FF_EXPERT), D_MODEL ** -0.5),
        'moe_w_up': nrm((N_MOE, N_EXPERTS, D_MODEL, D_FF_EXPERT), D_MODEL ** -0.5),
        'moe_w_down': nrm((N_MOE, N_EXPERTS, D_FF_EXPERT, D_MODEL), D_FF_EXPERT ** -0.5),
        'ln_final': gain((D_MODEL,)),
    }


def reference(x_prompt, x_sample, state_ssd, state_ssd_conv, state_lru, state_lru_conv,
              ln_mix, w_in, ssd_conv_w, ssd_conv_b, ssd_dt_bias, ssd_A_log, ssd_D, ssd_norm,
              lru_conv_w, lru_conv_b, lru_w_a, lru_b_a, lru_w_x, lru_b_x, lru_lambda, lru_norm,
              w_out, ln_ffn, ffn_w_gate, ffn_w_up, ffn_w_down,
              moe_w_router, moe_w_gate, moe_w_up, moe_w_down, ln_final):
    P = dict(ln_mix=ln_mix, w_in=w_in, ssd_conv_w=ssd_conv_w, ssd_conv_b=ssd_conv_b,
             ssd_dt_bias=ssd_dt_bias, ssd_A_log=ssd_A_log, ssd_D=ssd_D, ssd_norm=ssd_norm,
             lru_conv_w=lru_conv_w, lru_conv_b=lru_conv_b, lru_w_a=lru_w_a, lru_b_a=lru_b_a,
             lru_w_x=lru_w_x, lru_b_x=lru_b_x, lru_lambda=lru_lambda, lru_norm=lru_norm,
             w_out=w_out, ln_ffn=ln_ffn, ffn_w_gate=ffn_w_gate, ffn_w_up=ffn_w_up, ffn_w_down=ffn_w_down,
             moe_w_router=moe_w_router, moe_w_gate=moe_w_gate, moe_w_up=moe_w_up, moe_w_down=moe_w_down,
             ln_final=ln_final)
    y_prompt, ssd_p, ssd_conv_p, lru_p, lru_conv_p = run_trunk(x_prompt, None, P, True)
    y_sample, ssd_s, ssd_conv_s, lru_s, lru_conv_s = run_trunk(
        x_sample, (state_ssd, state_ssd_conv, state_lru, state_lru_conv), P, False)
    return (y_prompt, y_sample, ssd_p, ssd_conv_p, lru_p, lru_conv_p, ssd_s, ssd_conv_s, lru_s, lru_conv_s)
```

```python
import functools

import jax
import jax.numpy as jnp
from jax import lax
from jax.experimental import pallas as pl
from jax.experimental.pallas import tpu as pltpu

F32 = jnp.float32
BF16 = jnp.bfloat16
HIGHEST = lax.Precision.HIGHEST

D_MODEL = 1024
SSD_HEADS = 16
SSD_HEAD_DIM = 64
SSD_INNER = SSD_HEADS * SSD_HEAD_DIM
SSD_GROUPS = 2
SSD_STATE = 128
SSD_CONV_DIM = SSD_INNER + 2 * SSD_GROUPS * SSD_STATE
CONV_WIDTH = 4
CHUNK = 128
LRU_HEADS = 16
LRU_BLOCK = 64
LRU_WIDTH = LRU_HEADS * LRU_BLOCK
LRU_C = 8.0
N_EXPERTS = 8
EPS = 1e-6

LANES = 128
SUBLANES = 8
MXU_DIM = 256
VMEM_LIMIT = 56 * 1024 * 1024


def _params(semantics, vmem=VMEM_LIMIT):
    return pltpu.CompilerParams(dimension_semantics=semantics, vmem_limit_bytes=vmem)


def _const_spec(shape):
    nd = len(shape)
    return pl.BlockSpec(shape, lambda *_: (0,) * nd, pipeline_mode=pl.Buffered(1))


def _rmsnorm(x, g):
    return x * lax.rsqrt(jnp.mean(x * x, axis=-1, keepdims=True) + EPS) * g


def _sigmoid(x):
    return jax.nn.sigmoid(x)


def _silu(x):
    return x * _sigmoid(x)


def _softplus(x):
    return jnp.maximum(x, 0.0) + jnp.log1p(jnp.exp(-jnp.abs(x)))


def _dot(a, b):
    return jnp.dot(a, b, preferred_element_type=F32)


def _dot_nt(a, b):
    return lax.dot_general(a, b, (((1,), (1,)), ((), ())), preferred_element_type=F32)


def _dot_tn(a, b):
    return lax.dot_general(a, b, (((0,), (0,)), ((), ())), preferred_element_type=F32)


def _dot_exact(a, b):
    return jnp.dot(a, b, preferred_element_type=F32, precision=HIGHEST)


def _head_expand_matrix(rows, head_dim, width):
    r = lax.broadcasted_iota(jnp.int32, (rows, width), 0)
    c = lax.broadcasted_iota(jnp.int32, (rows, width), 1)
    shift = head_dim.bit_length() - 1
    return (lax.shift_right_logical(c, shift) == r).astype(F32)


def _in_proj_body(x_ref, g_ref, wz_ref, wxbc_ref, wxl_ref, wgate_ref, wdt_ref,
                  z_ref, xbc_ref, xl_ref, gate_ref, dt_ref):
    hn = _rmsnorm(x_ref[...], g_ref[...]).astype(BF16)
    z_ref[...] = _dot(hn, wz_ref[...])
    xbc_ref[...] = _dot(hn, wxbc_ref[...])
    xl_ref[...] = _dot(hn, wxl_ref[...])
    gate_ref[...] = _dot(hn, wgate_ref[...])
    dt_ref[...] = _dot(hn, wdt_ref[...])


def _in_proj(x, g, ws, tm):
    t = x.shape[0]
    widths = [w.shape[1] for w in ws]
    row = lambda n: pl.BlockSpec((tm, n), lambda i: (i, 0))
    return pl.pallas_call(
        _in_proj_body,
        grid=(t // tm,),
        in_specs=[row(D_MODEL), _const_spec((1, D_MODEL))] + [_const_spec(w.shape) for w in ws],
        out_specs=[row(n) for n in widths],
        out_shape=[jax.ShapeDtypeStruct((t, n), F32) for n in widths],
        compiler_params=_params(("parallel",)),
        name="in_proj",
    )(x, g, *ws)


def _route(logits):
    lane = lax.broadcasted_iota(jnp.int32, logits.shape, 1)
    neg = jnp.float32(-jnp.inf)
    l1 = jnp.where(lane < N_EXPERTS, logits, neg)
    m1 = jnp.max(l1, axis=-1, keepdims=True)
    i1 = jnp.min(jnp.where(l1 == m1, lane, LANES), axis=-1, keepdims=True)
    l2 = jnp.where(lane == i1, neg, l1)
    m2 = jnp.max(l2, axis=-1, keepdims=True)
    i2 = jnp.min(jnp.where(l2 == m2, lane, LANES), axis=-1, keepdims=True)
    e = jnp.exp(m2 - m1)
    g1 = 1.0 / (1.0 + e)
    g2 = e / (1.0 + e)
    dense = jnp.where(lane == i1, g1, 0.0) + jnp.where(lane == i2, g2, 0.0)
    info = (dense
            + jnp.where(lane == N_EXPERTS, i1.astype(F32), 0.0)
            + jnp.where(lane == N_EXPERTS + 1, i2.astype(F32), 0.0)
            + jnp.where(lane == N_EXPERTS + 2, g1, 0.0)
            + jnp.where(lane == N_EXPERTS + 3, g2, 0.0))
    return info


def _out_proj_body(ys_ref, yl_ref, x_ref, wt_ref, wb_ref, g_ref, *rest, moe):
    if moe:
        wr_ref, x1_ref, hn_ref, hnf_ref, route_ref = rest
    else:
        x1_ref, hn_ref = rest
    x1 = x_ref[...] + _dot(ys_ref[...], wt_ref[...]) + _dot(yl_ref[...], wb_ref[...])
    x1_ref[...] = x1
    hn = _rmsnorm(x1, g_ref[...])
    hn_ref[...] = hn.astype(BF16)
    if moe:
        hnf_ref[...] = hn
        route_ref[...] = _route(_dot_exact(hn, wr_ref[...]))


def _out_proj(ys, yl, x, wt, wb, g, wr, tm):
    t = x.shape[0]
    moe = wr is not None
    row = lambda n: pl.BlockSpec((tm, n), lambda i: (i, 0))
    in_specs = [row(SSD_INNER), row(LRU_WIDTH), row(D_MODEL),
                _const_spec(wt.shape), _const_spec(wb.shape), _const_spec((1, D_MODEL))]
    args = [ys, yl, x, wt, wb, g]
    out_specs = [row(D_MODEL), row(D_MODEL)]
    out_shape = [jax.ShapeDtypeStruct((t, D_MODEL), F32), jax.ShapeDtypeStruct((t, D_MODEL), BF16)]
    if moe:
        in_specs.append(_const_spec(wr.shape))
        args.append(wr)
        out_specs += [row(D_MODEL), row(LANES)]
        out_shape += [jax.ShapeDtypeStruct((t, D_MODEL), F32), jax.ShapeDtypeStruct((t, LANES), F32)]
    return pl.pallas_call(
        functools.partial(_out_proj_body, moe=moe),
        grid=(t // tm,),
        in_specs=in_specs,
        out_specs=out_specs,
        out_shape=out_shape,
        compiler_params=_params(("parallel",)),
        name="out_proj_moe" if moe else "out_proj",
    )(*args)


def _ffn_body(hn_ref, x1_ref, wg_ref, wu_ref, wd_ref, o_ref, *, tf):
    hn = hn_ref[...]
    o_ref[...] = x1_ref[...]
    d_ff = wg_ref.shape[1]
    for f in range(0, d_ff, tf):
        a = _silu(_dot(hn, wg_ref[:, f:f + tf])) * _dot(hn, wu_ref[:, f:f + tf])
        o_ref[...] += _dot(a.astype(BF16), wd_ref[f:f + tf, :])


def _ffn(hn, x1, wg, wu, wd, tm, tf):
    t = hn.shape[0]
    row = lambda n: pl.BlockSpec((tm, n), lambda i: (i, 0))
    return pl.pallas_call(
        functools.partial(_ffn_body, tf=tf),
        grid=(t // tm,),
        in_specs=[row(D_MODEL), row(D_MODEL), _const_spec(wg.shape), _const_spec(wu.shape), _const_spec(wd.shape)],
        out_specs=row(D_MODEL),
        out_shape=jax.ShapeDtypeStruct((t, D_MODEL), F32),
        compiler_params=_params(("parallel",)),
        name="ffn_dense",
    )(hn, x1, wg, wu, wd)


def _conv_chunk(x, xp_ref, w_ref, b_ref):
    q = x.shape[0]
    xp_ref[pl.ds(SUBLANES, q), :] = x
    y = b_ref[...] + x * w_ref[CONV_WIDTH - 1:CONV_WIDTH, :]
    for k in range(CONV_WIDTH - 1):
        y = y + xp_ref[pl.ds(SUBLANES - (CONV_WIDTH - 1) + k, q), :] * w_ref[k:k + 1, :]
    xp_ref[pl.ds(0, SUBLANES), :] = x[q - SUBLANES:, :]
    return y


def _ssd_prompt_body(xbc_ref, z_ref, dt_ref, cw_ref, cb_ref, dtb_ref, alog_ref, dskip_ref, ng_ref,
                     y_ref, hlast_ref, cstate_ref, h_ref, xp_ref, yacc_ref):
    c = pl.program_id(1)
    nc = pl.num_programs(1)
    q = CHUNK

    @pl.when(c == 0)
    def _():
        h_ref[...] = jnp.zeros_like(h_ref)
        xp_ref[pl.ds(0, SUBLANES), :] = jnp.zeros((SUBLANES, SSD_CONV_DIM), F32)

    x = xbc_ref[0]
    u = _silu(_conv_chunk(x, xp_ref, cw_ref, cb_ref))
    xs = u[:, :SSD_INNER]
    dtv = _softplus(dt_ref[0] + dtb_ref[...])
    a = dtv * (-jnp.exp(alog_ref[...]))
    r = lax.broadcasted_iota(jnp.int32, (q, q), 0)
    s = lax.broadcasted_iota(jnp.int32, (q, q), 1)
    tril = r >= s
    cum = _dot_exact(tril.astype(F32), a)
    cum_t = cum.T
    expand = _head_expand_matrix(LANES, SSD_HEAD_DIM, SSD_INNER)
    cum_last = cum[q - 1:q, :]
    dtx = _dot_exact(dtv, expand) * xs
    ecum = _dot_exact(jnp.exp(cum), expand)
    dend = _dot_exact(jnp.exp(cum_last - cum), expand)
    cdec = jnp.exp(cum_last)
    dtx_b = dtx.astype(BF16)
    dtxd_b = (dtx * dend).astype(BF16)

    heads_per_group = SSD_HEADS // SSD_GROUPS
    for g in range(SSD_GROUPS):
        bm = u[:, SSD_INNER + g * SSD_STATE:SSD_INNER + (g + 1) * SSD_STATE].astype(BF16)
        cm = u[:, SSD_INNER + (SSD_GROUPS + g) * SSD_STATE:SSD_INNER + (SSD_GROUPS + g + 1) * SSD_STATE].astype(BF16)
        cbm = _dot_nt(cm, bm)
        for j in range(heads_per_group):
            h = g * heads_per_group + j
            lo, hi = h * SSD_HEAD_DIM, (h + 1) * SSD_HEAD_DIM
            seg = cum[:, h:h + 1] - cum_t[h:h + 1, :]
            m = cbm * jnp.exp(jnp.where(tril, seg, -jnp.inf))
            hprev = h_ref[h]
            y_h = _dot(m.astype(BF16), dtx_b[:, lo:hi])
            y_h = y_h + _dot_nt(cm, hprev.astype(BF16)) * ecum[:, lo:hi]
            yacc_ref[:, lo:hi] = y_h
            st = _dot_tn(dtxd_b[:, lo:hi], bm)
            h_ref[h] = hprev * cdec[:, h:h + 1] + st

    y = yacc_ref[...] + dskip_ref[...] * xs
    y = y * _silu(z_ref[0])
    half = SSD_INNER // SSD_GROUPS
    parts = []
    for g in range(SSD_GROUPS):
        yg = y[:, g * half:(g + 1) * half]
        parts.append(yg * lax.rsqrt(jnp.mean(yg * yg, axis=-1, keepdims=True) + EPS))
    y = jnp.concatenate(parts, axis=-1) * ng_ref[...]
    y_ref[0] = y.astype(y_ref.dtype)

    @pl.when(c == nc - 1)
    def _():
        hlast_ref[0] = h_ref[...]
        cstate_ref[0] = xbc_ref[0, q - (CONV_WIDTH - 1):q, :]


def _ssd_prompt(xbc, z, dt, cw, cb, dtb, alog, dskip, ng):
    b, l, _ = xbc.shape
    nc = l // CHUNK
    blk = lambda n: pl.BlockSpec((1, CHUNK, n), lambda i, c: (i, c, 0))
    return pl.pallas_call(
        _ssd_prompt_body,
        grid=(b, nc),
        in_specs=[blk(SSD_CONV_DIM), blk(SSD_INNER), blk(LANES),
                  _const_spec(cw.shape), _const_spec(cb.shape), _const_spec(dtb.shape),
                  _const_spec(alog.shape), _const_spec(dskip.shape), _const_spec(ng.shape)],
        out_specs=[blk(SSD_INNER),
                   pl.BlockSpec((1, SSD_HEADS, SSD_HEAD_DIM, SSD_STATE), lambda i, c: (i, 0, 0, 0)),
                   pl.BlockSpec((1, CONV_WIDTH - 1, SSD_CONV_DIM), lambda i, c: (i, 0, 0))],
        out_shape=[jax.ShapeDtypeStruct((b, l, SSD_INNER), BF16),
                   jax.ShapeDtypeStruct((b, SSD_HEADS, SSD_HEAD_DIM, SSD_STATE), F32),
                   jax.ShapeDtypeStruct((b, CONV_WIDTH - 1, SSD_CONV_DIM), F32)],
        scratch_shapes=[pltpu.VMEM((SSD_HEADS, SSD_HEAD_DIM, SSD_STATE), F32),
                        pltpu.VMEM((SUBLANES + CHUNK, SSD_CONV_DIM), F32),
                        pltpu.VMEM((CHUNK, SSD_INNER), F32)],
        compiler_params=_params(("parallel", "arbitrary")),
        name="ssd_prompt",
    )(xbc, z, dt, cw, cb, dtb, alog, dskip, ng)


def _lru_gates(xc, wa_ref, wx_ref, ba_ref, bx_ref, lam_ref):
    xcb = xc.astype(BF16)
    n_tiles = LRU_WIDTH // MXU_DIM
    ra, ri = [], []
    for j in range(n_tiles):
        sl = slice(j * MXU_DIM, (j + 1) * MXU_DIM)
        ra.append(_dot(xcb[:, sl], wa_ref[j]))
        ri.append(_dot(xcb[:, sl], wx_ref[j]))
    r = _sigmoid(jnp.concatenate(ra, axis=-1) + ba_ref[...])
    i = _sigmoid(jnp.concatenate(ri, axis=-1) + bx_ref[...])
    log_a = -LRU_C * r * _softplus(-lam_ref[...])
    a = jnp.exp(log_a)
    mult = jnp.sqrt(-jnp.tanh(log_a) * (a * a + 1.0))
    return a, mult, i * xc


def _lru_prompt_body(xl_ref, gate_ref, cw_ref, cb_ref, wa_ref, wx_ref, ba_ref, bx_ref, lam_ref, ng_ref,
                     y_ref, hlast_ref, cstate_ref, hc_ref, xp_ref, a_ref, u_ref, *, tl):
    c = pl.program_id(1)
    nc = pl.num_programs(1)

    @pl.when(c == 0)
    def _():
        hc_ref[...] = jnp.zeros_like(hc_ref)
        xp_ref[pl.ds(0, SUBLANES), :] = jnp.zeros((SUBLANES, LRU_WIDTH), F32)

    xc = _conv_chunk(xl_ref[0], xp_ref, cw_ref, cb_ref)
    a, mult, ix = _lru_gates(xc, wa_ref, wx_ref, ba_ref, bx_ref, lam_ref)
    row = lax.broadcasted_iota(jnp.int32, (tl, LRU_WIDTH), 0)
    first = jnp.logical_and(row == 0, c == 0)
    u = jnp.where(first, 1.0, mult) * ix

    ng = tl // SUBLANES
    a3 = a.reshape(ng, SUBLANES, LRU_WIDTH)
    u3 = u.reshape(ng, SUBLANES, LRU_WIDTH)
    sub = lax.broadcasted_iota(jnp.int32, (ng, SUBLANES, LRU_WIDTH), 1)
    d = 1
    while d < SUBLANES:
        keep = sub >= d
        a_sh = jnp.where(keep, pltpu.roll(a3, d, 1), 1.0)
        u_sh = jnp.where(keep, pltpu.roll(u3, d, 1), 0.0)
        u3 = u3 + a3 * u_sh
        a3 = a3 * a_sh
        d *= 2
    a_ref[...] = a3.reshape(tl, LRU_WIDTH)
    u_ref[...] = u3.reshape(tl, LRU_WIDTH)

    def group(j, hin):
        off = pl.multiple_of(j * SUBLANES, SUBLANES)
        hj = u_ref[pl.ds(off, SUBLANES), :] + a_ref[pl.ds(off, SUBLANES), :] * hin
        u_ref[pl.ds(off, SUBLANES), :] = hj
        return jnp.broadcast_to(hj[SUBLANES - 1:SUBLANES, :], (SUBLANES, LRU_WIDTH))

    hc_ref[...] = lax.fori_loop(0, ng, group, hc_ref[...], unroll=4)
    h = u_ref[...]
    y = h * jax.nn.gelu(gate_ref[0])
    y = _rmsnorm(y, ng_ref[...])
    y_ref[0] = y.astype(y_ref.dtype)

    @pl.when(c == nc - 1)
    def _():
        hlast_ref[0] = hc_ref[0:1, :]
        cstate_ref[0] = xl_ref[0, tl - (CONV_WIDTH - 1):tl, :]


def _lru_prompt(xl, gate, cw, cb, wa, wx, ba, bx, lam, ng, tl):
    b, l, _ = xl.shape
    blk = lambda n: pl.BlockSpec((1, tl, n), lambda i, c: (i, c, 0))
    consts = [cw, cb, wa, wx, ba, bx, lam, ng]
    y, hlast, cstate = pl.pallas_call(
        functools.partial(_lru_prompt_body, tl=tl),
        grid=(b, l // tl),
        in_specs=[blk(LRU_WIDTH), blk(LRU_WIDTH)] + [_const_spec(w.shape) for w in consts],
        out_specs=[blk(LRU_WIDTH),
                   pl.BlockSpec((1, 1, LRU_WIDTH), lambda i, c: (i, 0, 0)),
                   pl.BlockSpec((1, CONV_WIDTH - 1, LRU_WIDTH), lambda i, c: (i, 0, 0))],
        out_shape=[jax.ShapeDtypeStruct((b, l, LRU_WIDTH), BF16),
                   jax.ShapeDtypeStruct((b, 1, LRU_WIDTH), F32),
                   jax.ShapeDtypeStruct((b, CONV_WIDTH - 1, LRU_WIDTH), F32)],
        scratch_shapes=[pltpu.VMEM((SUBLANES, LRU_WIDTH), F32),
                        pltpu.VMEM((SUBLANES + tl, LRU_WIDTH), F32),
                        pltpu.VMEM((tl, LRU_WIDTH), F32),
                        pltpu.VMEM((tl, LRU_WIDTH), F32)],
        compiler_params=_params(("parallel", "arbitrary")),
        name="lru_prompt",
    )(xl, gate, *consts)
    return y, hlast[:, 0, :], cstate


def _conv_step(x_ref, c0_ref, c1_ref, c2_ref, w_ref, b_ref):
    return (b_ref[...] + c0_ref[...] * w_ref[0:1, :] + c1_ref[...] * w_ref[1:2, :]
            + c2_ref[...] * w_ref[2:3, :] + x_ref[...] * w_ref[3:4, :])


def _ssd_step_pre_body(xbc_ref, c0_ref, c1_ref, c2_ref, dt_ref, cw_ref, cb_ref, dtb_ref, alog_ref,
                       u_ref, dte_ref, dece_ref, cbe_ref, dts_ref, decs_ref, xt_ref):
    u = _silu(_conv_step(xbc_ref, c0_ref, c1_ref, c2_ref, cw_ref, cb_ref))
    u_ref[...] = u
    dtv = _softplus(dt_ref[...] + dtb_ref[...])
    dec = jnp.exp(dtv * (-jnp.exp(alog_ref[...])))
    dts_ref[...] = dtv
    decs_ref[...] = dec
    expand = _head_expand_matrix(LANES, SSD_HEAD_DIM, SSD_INNER)
    dte_ref[...] = _dot_exact(dtv, expand)
    dece_ref[...] = _dot_exact(dec, expand)
    half = SSD_INNER // SSD_GROUPS
    parts = []
    for g in range(SSD_GROUPS):
        bm = u[:, SSD_INNER + g * SSD_STATE:SSD_INNER + (g + 1) * SSD_STATE]
        cm = u[:, SSD_INNER + (SSD_GROUPS + g) * SSD_STATE:SSD_INNER + (SSD_GROUPS + g + 1) * SSD_STATE]
        cbg = jnp.sum(cm * bm, axis=-1, keepdims=True)
        parts.append(jnp.broadcast_to(cbg, (u.shape[0], half)))
    cbe_ref[...] = jnp.concatenate(parts, axis=-1)
    xt_ref[...] = u[:, :SSD_INNER].T.astype(BF16)


def _ssd_step_pre(xbc, c0, c1, c2, dt, cw, cb, dtb, alog):
    n = xbc.shape[0]
    args = [xbc, c0, c1, c2, dt, cw, cb, dtb, alog]
    full = lambda shape: pl.BlockSpec(shape, lambda i: (0,) * len(shape))
    out_shape = [jax.ShapeDtypeStruct((n, SSD_CONV_DIM), F32),
                 jax.ShapeDtypeStruct((n, SSD_INNER), F32),
                 jax.ShapeDtypeStruct((n, SSD_INNER), F32),
                 jax.ShapeDtypeStruct((n, SSD_INNER), F32),
                 jax.ShapeDtypeStruct((n, LANES), F32),
                 jax.ShapeDtypeStruct((n, LANES), F32),
                 jax.ShapeDtypeStruct((SSD_INNER, n), BF16)]
    return pl.pallas_call(
        _ssd_step_pre_body,
        grid=(1,),
        in_specs=[full(a.shape) for a in args],
        out_specs=[full(s.shape) for s in out_shape],
        out_shape=out_shape,
        compiler_params=_params(("arbitrary",)),
        name="ssd_step_pre",
    )(*args)


def _ssd_step_body(dts_ref, decs_ref, h0_ref, u_ref, xt_ref, ublk_ref, dte_ref, dece_ref, cbe_ref, z_ref,
                   dskip_ref, ng_ref, y_ref, h1_ref, r_ref, *, bb, n):
    i = pl.program_id(0)
    heads_per_group = SSD_HEADS // SSD_GROUPS
    half = SSD_INNER // SSD_GROUPS
    sel_r = lax.broadcasted_iota(jnp.int32, (n, SSD_STATE), 0)

    def one(j, carry):
        b = i * bb + j
        xcol = _dot(xt_ref[...], (sel_r == b).astype(BF16))
        urow = u_ref[pl.ds(b, 1), :]
        rparts = []
        for g in range(SSD_GROUPS):
            brow = urow[:, SSD_INNER + g * SSD_STATE:SSD_INNER + (g + 1) * SSD_STATE]
            crow = urow[:, SSD_INNER + (SSD_GROUPS + g) * SSD_STATE:SSD_INNER + (SSD_GROUPS + g + 1) * SSD_STATE]
            c8 = jnp.broadcast_to(crow, (SUBLANES, SSD_STATE)).astype(BF16)
            hg = h0_ref[j, g * heads_per_group:(g + 1) * heads_per_group]
            hg2 = hg.reshape(half, SSD_STATE)
            rparts.append(_dot_nt(c8, hg2.astype(BF16)))
            for k in range(heads_per_group):
                h = g * heads_per_group + k
                dt_s = dts_ref[b * SSD_HEADS + h]
                dec_s = decs_ref[b * SSD_HEADS + h]
                xc = xcol[h * SSD_HEAD_DIM:(h + 1) * SSD_HEAD_DIM, :]
                h1_ref[j, h] = h0_ref[j, h] * dec_s + (dt_s * xc) * brow
        r_ref[pl.ds(j, 1), :] = jnp.concatenate(rparts, axis=-1)[0:1, :]
        return carry

    lax.fori_loop(0, bb, one, 0)
    xs = ublk_ref[:, :SSD_INNER]
    y = dece_ref[...] * r_ref[...] + dte_ref[...] * xs * cbe_ref[...] + dskip_ref[...] * xs
    y = y * _silu(z_ref[...])
    parts = []
    for g in range(SSD_GROUPS):
        yg = y[:, g * half:(g + 1) * half]
        parts.append(yg * lax.rsqrt(jnp.mean(yg * yg, axis=-1, keepdims=True) + EPS))
    y_ref[...] = (jnp.concatenate(parts, axis=-1) * ng_ref[...]).astype(y_ref.dtype)


def _ssd_step(dts, decs, h0, u, xt, dte, dece, cbe, z, dskip, ng, bb):
    n = h0.shape[0]
    row = lambda w: pl.BlockSpec((bb, w), lambda i, *_: (i, 0))
    full = lambda shape: pl.BlockSpec(shape, lambda i, *_: (0,) * len(shape))
    st = pl.BlockSpec((bb, SSD_HEADS, SSD_HEAD_DIM, SSD_STATE), lambda i, *_: (i, 0, 0, 0))
    grid_spec = pltpu.PrefetchScalarGridSpec(
        num_scalar_prefetch=2,
        grid=(n // bb,),
        in_specs=[st, full(u.shape), full(xt.shape), row(SSD_CONV_DIM), row(SSD_INNER), row(SSD_INNER),
                  row(SSD_INNER), row(SSD_INNER), full(dskip.shape), full(ng.shape)],
        out_specs=[row(SSD_INNER), st],
        scratch_shapes=[pltpu.VMEM((bb, SSD_INNER), F32)],
    )
    return pl.pallas_call(
        functools.partial(_ssd_step_body, bb=bb, n=n),
        grid_spec=grid_spec,
        out_shape=[jax.ShapeDtypeStruct((n, SSD_INNER), BF16),
                   jax.ShapeDtypeStruct(h0.shape, F32)],
        compiler_params=_params(("arbitrary",)),
        name="ssd_step",
    )(dts, decs, h0, u, xt, u, dte, dece, cbe, z, dskip, ng)


def _lru_step_body(xl_ref, c0_ref, c1_ref, c2_ref, gate_ref, h0_ref, cw_ref, cb_ref, wa_ref, wx_ref,
                   ba_ref, bx_ref, lam_ref, ng_ref, y_ref, h1_ref):
    xc = _conv_step(xl_ref, c0_ref, c1_ref, c2_ref, cw_ref, cb_ref)
    a, mult, ix = _lru_gates(xc, wa_ref, wx_ref, ba_ref, bx_ref, lam_ref)
    h = mult * ix + a * h0_ref[...]
    h1_ref[...] = h
    y = h * jax.nn.gelu(gate_ref[...])
    y_ref[...] = _rmsnorm(y, ng_ref[...]).astype(y_ref.dtype)


def _lru_step(xl, c0, c1, c2, gate, h0, cw, cb, wa, wx, ba, bx, lam, ng):
    n = xl.shape[0]
    args = [xl, c0, c1, c2, gate, h0, cw, cb, wa, wx, ba, bx, lam, ng]
    full = lambda shape: pl.BlockSpec(shape, lambda i: (0,) * len(shape))
    return pl.pallas_call(
        _lru_step_body,
        grid=(1,),
        in_specs=[full(a.shape) for a in args],
        out_specs=[full((n, LRU_WIDTH)), full((n, LRU_WIDTH))],
        out_shape=[jax.ShapeDtypeStruct((n, LRU_WIDTH), BF16), jax.ShapeDtypeStruct((n, LRU_WIDTH), F32)],
        compiler_params=_params(("arbitrary",)),
        name="lru_step",
    )(*args)


def _moe_dense_body(hn_ref, x1_ref, route_ref, wg_ref, wu_ref, wd_ref, o_ref):
    e = pl.program_id(0)
    f = pl.program_id(1)

    @pl.when(jnp.logical_and(e == 0, f == 0))
    def _():
        o_ref[...] = x1_ref[...]

    hn = hn_ref[...]
    a = _silu(_dot(hn, wg_ref[...])) * _dot(hn, wu_ref[...])
    lane = lax.broadcasted_iota(jnp.int32, route_ref.shape, 1)
    gate = jnp.sum(jnp.where(lane == e, route_ref[...], 0.0), axis=-1, keepdims=True)
    o_ref[...] += gate * _dot(a.astype(BF16), wd_ref[...])


def _moe_dense(hn, x1, route, wg, wu, wd, tf):
    n = hn.shape[0]
    d_ff = wg.shape[2]
    full = lambda w: pl.BlockSpec((n, w), lambda e, f: (0, 0))
    return pl.pallas_call(
        _moe_dense_body,
        grid=(N_EXPERTS, d_ff // tf),
        in_specs=[full(D_MODEL), full(D_MODEL), full(LANES),
                  pl.BlockSpec((None, D_MODEL, tf), lambda e, f: (e, 0, f)),
                  pl.BlockSpec((None, D_MODEL, tf), lambda e, f: (e, 0, f)),
                  pl.BlockSpec((None, tf, D_MODEL), lambda e, f: (e, f, 0))],
        out_specs=full(D_MODEL),
        out_shape=jax.ShapeDtypeStruct((n, D_MODEL), F32),
        compiler_params=_params(("arbitrary", "arbitrary")),
        name="moe_dense",
    )(hn, x1, route, wg, wu, wd)


def _moe_grouped_body(be_ref, nv_ref, tok_ref, hn_hbm, gate_ref, wg_ref, wu_ref, wd_ref, o_ref,
                      xbuf_ref, xb_ref, acc_ref, sem_ref, *, tm):
    i = pl.program_id(0)
    f = pl.program_id(1)
    nf = pl.num_programs(1)
    nvalid = nv_ref[0]
    slot = lax.rem(i, 2)

    def row_copy(t, r, s):
        return pltpu.make_async_copy(hn_hbm.at[pl.ds(t, 1)], xbuf_ref.at[s, pl.ds(r, 1)], sem_ref.at[s])

    def start_gather(blk, s):
        def body(r, c):
            row_copy(tok_ref[blk * tm + r], r, s).start()
            return c
        lax.fori_loop(0, tm, body, 0, unroll=8)

    def wait_gather(s):
        def body(r, c):
            row_copy(0, r, s).wait()
            return c
        lax.fori_loop(0, tm, body, 0, unroll=8)

    valid = i < nvalid

    @pl.when(f == 0)
    def _():
        acc_ref[...] = jnp.zeros_like(acc_ref)

        @pl.when(jnp.logical_and(i == 0, valid))
        def _():
            start_gather(0, 0)

        @pl.when(valid)
        def _():
            wait_gather(slot)

        @pl.when(i + 1 < nvalid)
        def _():
            start_gather(i + 1, 1 - slot)

        @pl.when(valid)
        def _():
            xb_ref[...] = xbuf_ref[slot].astype(BF16)

    @pl.when(valid)
    def _():
        x = xb_ref[...]
        a = _silu(_dot(x, wg_ref[...])) * _dot(x, wu_ref[...])
        acc_ref[...] += _dot(a.astype(BF16), wd_ref[...])

    @pl.when(f == nf - 1)
    def _():
        o_ref[...] = acc_ref[...] * gate_ref[...]


def _moe_grouped(blk_e, nvalid, tok, hn, gate_sorted, wg, wu, wd, tm, tf):
    nblk = blk_e.shape[0]
    d_ff = wg.shape[2]
    nf = d_ff // tf

    def wcol(i, f, be, nv, tk):
        return (be[i], 0, jnp.where(i < nv[0], f, nf - 1))

    def wrow(i, f, be, nv, tk):
        return (be[i], jnp.where(i < nv[0], f, nf - 1), 0)

    grid_spec = pltpu.PrefetchScalarGridSpec(
        num_scalar_prefetch=3,
        grid=(nblk, nf),
        in_specs=[pl.BlockSpec(memory_space=pl.ANY),
                  pl.BlockSpec((tm, 1), lambda i, f, *_: (i, 0)),
                  pl.BlockSpec((None, D_MODEL, tf), wcol),
                  pl.BlockSpec((None, D_MODEL, tf), wcol),
                  pl.BlockSpec((None, tf, D_MODEL), wrow)],
        out_specs=pl.BlockSpec((tm, D_MODEL), lambda i, f, *_: (i, 0)),
        scratch_shapes=[pltpu.VMEM((2, tm, D_MODEL), F32),
                        pltpu.VMEM((tm, D_MODEL), BF16),
                        pltpu.VMEM((tm, D_MODEL), F32),
                        pltpu.SemaphoreType.DMA((2,))],
    )
    return pl.pallas_call(
        functools.partial(_moe_grouped_body, tm=tm),
        grid_spec=grid_spec,
        out_shape=jax.ShapeDtypeStruct((nblk * tm, D_MODEL), F32),
        compiler_params=_params(("arbitrary", "arbitrary")),
        name="moe_grouped",
    )(blk_e, nvalid, tok, hn, gate_sorted, wg, wu, wd)


def _moe_combine_body(slot_ref, x1_ref, ys_hbm, *rest, tc, final):
    if final:
        g_ref, o_ref, buf_ref, sem_ref = rest
    else:
        o_ref, buf_ref, sem_ref = rest
    i = pl.program_id(0)
    n = pl.num_programs(0)
    s = lax.rem(i, 2)

    def row_copy(src, k, r, b):
        return pltpu.make_async_copy(ys_hbm.at[pl.ds(src, 1)], buf_ref.at[b, k, pl.ds(r, 1)], sem_ref.at[b])

    def start_gather(blk, b):
        def body(r, c):
            t = blk * tc + r
            row_copy(slot_ref[2 * t], 0, r, b).start()
            row_copy(slot_ref[2 * t + 1], 1, r, b).start()
            return c
        lax.fori_loop(0, tc, body, 0, unroll=8)

    def wait_gather(b):
        def body(r, c):
            row_copy(0, 0, r, b).wait()
            row_copy(0, 1, r, b).wait()
            return c
        lax.fori_loop(0, tc, body, 0, unroll=8)

    @pl.when(i == 0)
    def _():
        start_gather(0, 0)

    wait_gather(s)

    @pl.when(i + 1 < n)
    def _():
        start_gather(i + 1, 1 - s)

    x2 = x1_ref[...] + buf_ref[s, 0] + buf_ref[s, 1]
    if final:
        x2 = _rmsnorm(x2, g_ref[...])
    o_ref[...] = x2


def _moe_combine(slots, x1, ys, g_final, tc):
    t = x1.shape[0]
    final = g_final is not None
    in_specs = [pl.BlockSpec((tc, D_MODEL), lambda i, *_: (i, 0)), pl.BlockSpec(memory_space=pl.ANY)]
    args = [x1, ys]
    if final:
        in_specs.append(pl.BlockSpec((1, D_MODEL), lambda i, *_: (0, 0)))
        args.append(g_final)
    grid_spec = pltpu.PrefetchScalarGridSpec(
        num_scalar_prefetch=1,
        grid=(t // tc,),
        in_specs=in_specs,
        out_specs=pl.BlockSpec((tc, D_MODEL), lambda i, *_: (i, 0)),
        scratch_shapes=[pltpu.VMEM((2, 2, tc, D_MODEL), F32), pltpu.SemaphoreType.DMA((2,))],
    )
    return pl.pallas_call(
        functools.partial(_moe_combine_body, tc=tc, final=final),
        grid_spec=grid_spec,
        out_shape=jax.ShapeDtypeStruct((t, D_MODEL), F32),
        compiler_params=_params(("arbitrary",)),
        name="moe_combine_final" if final else "moe_combine",
    )(slots, *args)


def _final_norm_body(x_ref, g_ref, o_ref):
    o_ref[...] = _rmsnorm(x_ref[...], g_ref[...])


def _final_norm(x, g, tm):
    t = x.shape[0]
    return pl.pallas_call(
        _final_norm_body,
        grid=(t // tm,),
        in_specs=[pl.BlockSpec((tm, D_MODEL), lambda i: (i, 0)), pl.BlockSpec((1, D_MODEL), lambda i: (0, 0))],
        out_specs=pl.BlockSpec((tm, D_MODEL), lambda i: (i, 0)),
        out_shape=jax.ShapeDtypeStruct((t, D_MODEL), F32),
        compiler_params=_params(("parallel",)),
        name="final_norm",
    )(x, g)


def _route_tables(route, tm, nblk):
    t = route.shape[0]
    e = route[:, N_EXPERTS:N_EXPERTS + 2].astype(jnp.int32).reshape(-1)
    g = route[:, N_EXPERTS + 2:N_EXPERTS + 4].reshape(-1)
    onehot = (e[:, None] == jnp.arange(N_EXPERTS, dtype=jnp.int32)[None, :]).astype(jnp.int32)
    csum = jnp.cumsum(onehot, axis=0)
    rank = jnp.sum(csum * onehot, axis=1) - 1
    counts = csum[-1]
    padded = ((counts + tm - 1) // tm) * tm
    ends = jnp.cumsum(padded)
    starts = ends - padded
    slots = jnp.sum(starts[None, :] * onehot, axis=1) + rank
    tok = jnp.zeros((nblk * tm,), jnp.int32).at[slots].set(jnp.arange(2 * t, dtype=jnp.int32) // 2)
    gate_sorted = jnp.zeros((nblk * tm,), F32).at[slots].set(g)
    nvalid = (ends[-1] // tm).astype(jnp.int32)
    blk_start = jnp.minimum(jnp.arange(nblk, dtype=jnp.int32), nvalid - 1) * tm
    blk_e = jnp.sum((blk_start[:, None] >= ends[None, :]).astype(jnp.int32), axis=1)
    blk_e = jnp.minimum(blk_e, N_EXPERTS - 1).astype(jnp.int32)
    return blk_e, nvalid.reshape(1), tok, gate_sorted.reshape(-1, 1), slots.astype(jnp.int32)


def _block_diag_tiles(w):
    per = MXU_DIM // LRU_BLOCK
    n_tiles = LRU_HEADS // per
    w = w.reshape(n_tiles, per, LRU_BLOCK, LRU_BLOCK)
    eye = jnp.eye(per, dtype=w.dtype)
    t = jnp.einsum('tpij,pq->tpiqj', w, eye)
    return t.reshape(n_tiles, MXU_DIM, MXU_DIM).astype(BF16)


def _prep_layer(l, p):
    w_in = p['w_in'][l]
    o = 0
    cols = {}
    for name, width in (('z', SSD_INNER), ('xbc', SSD_CONV_DIM), ('dt', SSD_HEADS), ('xl', LRU_WIDTH), ('gate', LRU_WIDTH)):
        cols[name] = w_in[:, o:o + width]
        o += width
    w_dt = jnp.pad(cols['dt'], ((0, 0), (0, LANES - SSD_HEADS)))
    pad_heads = lambda v: jnp.pad(v, (0, LANES - SSD_HEADS)).reshape(1, LANES)
    row = lambda v: v.reshape(1, -1)
    w_out = p['w_out'][l].astype(BF16)
    return dict(
        ln_mix=row(p['ln_mix'][l]),
        w_in=[cols['z'].astype(BF16), cols['xbc'].astype(BF16), cols['xl'].astype(BF16),
              cols['gate'].astype(BF16), w_dt.astype(BF16)],
        ssd_conv_w=p['ssd_conv_w'][l], ssd_conv_b=row(p['ssd_conv_b'][l]),
        dt_bias=pad_heads(p['ssd_dt_bias'][l]), a_log=pad_heads(p['ssd_A_log'][l]),
        d_skip=row(jnp.repeat(p['ssd_D'][l], SSD_HEAD_DIM)), ssd_norm=row(p['ssd_norm'][l]),
        lru_conv_w=p['lru_conv_w'][l], lru_conv_b=row(p['lru_conv_b'][l]),
        lru_w_a=_block_diag_tiles(p['lru_w_a'][l]), lru_w_x=_block_diag_tiles(p['lru_w_x'][l]),
        lru_b_a=row(p['lru_b_a'][l]), lru_b_x=row(p['lru_b_x'][l]),
        lru_lambda=row(p['lru_lambda'][l]), lru_norm=row(p['lru_norm'][l]),
        w_out_ssd=w_out[:SSD_INNER], w_out_lru=w_out[SSD_INNER:],
        ln_ffn=row(p['ln_ffn'][l]),
    )


TM_PROMPT = 512
TM_EXPERT = 512
TF_EXPERT = 512
TF_DENSE = 256
TL_LRU = 256
TC_COMBINE = 256
BB_SSD_STEP = 8


def _channel_mixer(l, p, hn, hnf, route, x1, ffn_bf16, moe_bf16, prompt, final_g):
    k = l // 2
    if l % 2 == 0:
        wg, wu, wd = ffn_bf16[k]
        tm = TM_PROMPT if prompt else x1.shape[0]
        return _ffn(hn, x1, wg, wu, wd, tm, TF_DENSE)
    wg, wu, wd = moe_bf16[k]
    if not prompt:
        x2 = _moe_dense(hn, x1, route, wg, wu, wd, TF_EXPERT)
        return x2
    t = x1.shape[0]
    nblk = (2 * t) // TM_EXPERT + N_EXPERTS - 1
    blk_e, nvalid, tok, gate_sorted, slots = _route_tables(route, TM_EXPERT, nblk)
    ys = _moe_grouped(blk_e, nvalid, tok, hnf, gate_sorted, wg, wu, wd, TM_EXPERT, TF_EXPERT)
    return _moe_combine(slots, x1, ys, final_g, TC_COMBINE)


def _trunk(x, states, layers, p, ffn_bf16, moe_bf16, prompt):
    b, l_seq, _ = x.shape
    t = b * l_seq
    depth = len(layers)
    x = x.reshape(t, D_MODEL)
    tm = TM_PROMPT if prompt else t
    new_ssd, new_ssd_conv, new_lru, new_lru_conv = [], [], [], []
    ln_final = p['ln_final'].reshape(1, D_MODEL)
    for l in range(depth):
        w = layers[l]
        z, xbc, xl, gate, dt = _in_proj(x, w['ln_mix'], w['w_in'], tm)
        if prompt:
            y_ssd, hs, cs = _ssd_prompt(xbc.reshape(b, l_seq, -1), z.reshape(b, l_seq, -1), dt.reshape(b, l_seq, -1),
                                        w['ssd_conv_w'], w['ssd_conv_b'], w['dt_bias'], w['a_log'],
                                        w['d_skip'], w['ssd_norm'])
            y_lru, hl, cl = _lru_prompt(xl.reshape(b, l_seq, -1), gate.reshape(b, l_seq, -1),
                                        w['lru_conv_w'], w['lru_conv_b'], w['lru_w_a'], w['lru_w_x'],
                                        w['lru_b_a'], w['lru_b_x'], w['lru_lambda'], w['lru_norm'], TL_LRU)
            y_ssd = y_ssd.reshape(t, -1)
            y_lru = y_lru.reshape(t, -1)
        else:
            h0, c_ssd, hl0, c_lru = states[0][l], states[1][l], states[2][l], states[3][l]
            u, dte, dece, cbe, dts, decs, xt = _ssd_step_pre(
                xbc, c_ssd[:, 0], c_ssd[:, 1], c_ssd[:, 2], dt,
                w['ssd_conv_w'], w['ssd_conv_b'], w['dt_bias'], w['a_log'])
            y_ssd, hs = _ssd_step(dts[:, :SSD_HEADS].reshape(-1), decs[:, :SSD_HEADS].reshape(-1), h0, u, xt,
                                  dte, dece, cbe, z, w['d_skip'], w['ssd_norm'], BB_SSD_STEP)
            cs = jnp.stack([c_ssd[:, 1], c_ssd[:, 2], xbc], axis=1)
            y_lru, hl = _lru_step(xl, c_lru[:, 0], c_lru[:, 1], c_lru[:, 2], gate, hl0,
                                  w['lru_conv_w'], w['lru_conv_b'], w['lru_w_a'], w['lru_w_x'],
                                  w['lru_b_a'], w['lru_b_x'], w['lru_lambda'], w['lru_norm'])
            cl = jnp.stack([c_lru[:, 1], c_lru[:, 2], xl], axis=1)
        moe = l % 2 == 1
        wr = None
        if moe:
            wr = jnp.pad(p['moe_w_router'][l // 2], ((0, 0), (0, LANES - N_EXPERTS)))
        outs = _out_proj(y_ssd, y_lru, x, w['w_out_ssd'], w['w_out_lru'], w['ln_ffn'], wr, tm)
        if moe:
            x1, hn, hnf, route = outs
        else:
            (x1, hn), hnf, route = outs, None, None
        last = l == depth - 1
        fuse_final = last and moe and prompt
        x = _channel_mixer(l, p, hn, hnf, route, x1, ffn_bf16, moe_bf16, prompt, ln_final if fuse_final else None)
        if last and not fuse_final:
            x = _final_norm(x, ln_final, tm)
        new_ssd.append(hs)
        new_ssd_conv.append(cs)
        new_lru.append(hl)
        new_lru_conv.append(cl)
    y = x.reshape(b, l_seq, D_MODEL)
    return y, jnp.stack(new_ssd), jnp.stack(new_ssd_conv), jnp.stack(new_lru), jnp.stack(new_lru_conv)


def kernel(x_prompt, x_sample, state_ssd, state_ssd_conv, state_lru, state_lru_conv, ln_mix, w_in, ssd_conv_w, ssd_conv_b, ssd_dt_bias, ssd_A_log, ssd_D, ssd_norm, lru_conv_w, lru_conv_b, lru_w_a, lru_b_a, lru_w_x, lru_b_x, lru_lambda, lru_norm, w_out, ln_ffn, ffn_w_gate, ffn_w_up, ffn_w_down, moe_w_router, moe_w_gate, moe_w_up, moe_w_down, ln_final):
    p = dict(ln_mix=ln_mix, w_in=w_in, ssd_conv_w=ssd_conv_w, ssd_conv_b=ssd_conv_b,
             ssd_dt_bias=ssd_dt_bias, ssd_A_log=ssd_A_log, ssd_D=ssd_D, ssd_norm=ssd_norm,
             lru_conv_w=lru_conv_w, lru_conv_b=lru_conv_b, lru_w_a=lru_w_a, lru_b_a=lru_b_a,
             lru_w_x=lru_w_x, lru_b_x=lru_b_x, lru_lambda=lru_lambda, lru_norm=lru_norm,
             w_out=w_out, ln_ffn=ln_ffn, moe_w_router=moe_w_router, ln_final=ln_final)
    depth = w_in.shape[0]
    layers = [_prep_layer(l, p) for l in range(depth)]
    ffn_bf16 = [(ffn_w_gate[k].astype(BF16), ffn_w_up[k].astype(BF16), ffn_w_down[k].astype(BF16))
                for k in range(ffn_w_gate.shape[0])]
    moe_bf16 = [(moe_w_gate[k].astype(BF16), moe_w_up[k].astype(BF16), moe_w_down[k].astype(BF16))
                for k in range(moe_w_gate.shape[0])]
    y_p, ssd_p, ssd_conv_p, lru_p, lru_conv_p = _trunk(x_prompt, None, layers, p, ffn_bf16, moe_bf16, True)
    states = (state_ssd, state_ssd_conv, state_lru, state_lru_conv)
    y_s, ssd_s, ssd_conv_s, lru_s, lru_conv_s = _trunk(x_sample, states, layers, p, ffn_bf16, moe_bf16, False)
    return (y_p, y_s, ssd_p, ssd_conv_p, lru_p, lru_conv_p, ssd_s, ssd_conv_s, lru_s, lru_conv_s)
```
